```python
import jax, jax.numpy as jnp
from jax import lax
import numpy as np

D_MODEL = 1024
BATCH = 16
SEQ = 2048
DEPTH = 2

CHUNK = 64
N_A_LAYERS = DEPTH // 2
N_B_LAYERS = DEPTH - N_A_LAYERS
GMLP_BLOCK = 128
GMLP_WIDTH = D_MODEL
GMLP_GROUPS = 8
GMLP_GROUP_DIM = GMLP_WIDTH // GMLP_GROUPS
N_HEADS = 16
HEAD_DIM = D_MODEL // N_HEADS
LEFT_CHUNKS = 8
BAND = (LEFT_CHUNKS + 1) * CHUNK
PAD = LEFT_CHUNKS * CHUNK
REL_CLIP = 128
REL_SIZE = 2 * REL_CLIP + 1
D_FF = ((8 * D_MODEL // 3 + 127) // 128) * 128
CONV_WIDTH = 3
EPS = 1e-6
NEG_INF = -1e30

kernel_name = "hybrid_gmlp_chunkattn_yoco_convffn"


def rmsnorm(x, g):
    xf = x.astype(jnp.float32)
    y = xf * lax.rsqrt(jnp.mean(xf * xf, axis=-1, keepdims=True) + EPS)
    return (y * g.astype(jnp.float32)).astype(x.dtype)


def gmlp_mixer(h, w_in, v_norm_g, w_s, b_s, w_out):
    B, S, _ = h.shape
    z = jax.nn.gelu(h @ w_in)
    u, v = jnp.split(z, 2, axis=-1)
    v = rmsnorm(v, v_norm_g)
    pos_chunk = jnp.arange(GMLP_BLOCK) // CHUNK
    mask = pos_chunk[:, None] >= pos_chunk[None, :]
    w = jnp.where(mask[None], w_s, 0)
    v = v.reshape(B, S // GMLP_BLOCK, GMLP_BLOCK, GMLP_GROUPS, GMLP_GROUP_DIM)
    s = jnp.einsum('gij,bnjgc->bnigc', w, v) + b_s.T[None, None, :, :, None]
    out = u * s.reshape(B, S, GMLP_WIDTH)
    return out @ w_out


def chunk_attention(h, k, v, w_q, rel_bias, w_o):
    B, S, _ = h.shape
    nc = S // CHUNK
    scale = HEAD_DIM ** -0.5
    q = (h @ w_q).reshape(B, nc, CHUNK, N_HEADS, HEAD_DIM) * scale
    qc = jnp.moveaxis(q, 1, 0)
    kp = jnp.pad(k, ((0, 0), (PAD, 0), (0, 0), (0, 0)))
    vp = jnp.pad(v, ((0, 0), (PAD, 0), (0, 0), (0, 0)))
    qi = jnp.arange(CHUNK)[:, None]
    kj = jnp.arange(BAND)[None, :]
    rel_idx = jnp.clip(qi - kj + PAD, -REL_CLIP, REL_CLIP) + REL_CLIP
    bias = rel_bias[:, rel_idx].astype(jnp.float32)

    def one_chunk(args):
        c, qb = args
        start = c * CHUNK
        kb = lax.dynamic_slice_in_dim(kp, start, BAND, axis=1)
        vb = lax.dynamic_slice_in_dim(vp, start, BAND, axis=1)
        sc = jnp.einsum('bqhd,bkhd->bhqk', qb, kb).astype(jnp.float32) + bias
        valid = (start - PAD + jnp.arange(BAND)) >= 0
        sc = jnp.where(valid[None, None, None, :], sc, NEG_INF)
        p = jax.nn.softmax(sc, axis=-1).astype(vb.dtype)
        return jnp.einsum('bhqk,bkhd->bqhd', p, vb)

    o = lax.map(one_chunk, (jnp.arange(nc), qc))
    o = jnp.moveaxis(o, 0, 1).reshape(B, S, N_HEADS * HEAD_DIM)
    return o @ w_o


def conv_ffn(h, w_in, conv_w, conv_b, w_down):
    a = h @ w_in
    C = a.shape[-1]
    a = lax.conv_general_dilated(
        a, conv_w[:, None, :].astype(a.dtype), window_strides=(1,),
        padding=[(CONV_WIDTH - 1, 0)], dimension_numbers=('NWC', 'WIO', 'NWC'),
        feature_group_count=C) + conv_b
    up, gate = jnp.split(a, 2, axis=-1)
    return (jax.nn.silu(gate) * up) @ w_down


def setup_inputs(seed: int = 0) -> dict:
    key = jax.random.key(seed)
    ks = jax.random.split(key, 20)
    nrm = lambda k, shape, s: jax.random.normal(k, shape, jnp.float32) * s
    gain = lambda k, shape: 1.0 + nrm(k, shape, 0.02)
    HD = N_HEADS * HEAD_DIM
    return {
        "x": nrm(ks[0], (BATCH, SEQ, D_MODEL), 1.0),
        "a_norm_g": gain(ks[1], (N_A_LAYERS, D_MODEL)),
        "a_w_in": nrm(ks[2], (N_A_LAYERS, D_MODEL, 2 * GMLP_WIDTH), D_MODEL ** -0.5),
        "a_v_norm_g": gain(ks[3], (N_A_LAYERS, GMLP_WIDTH)),
        "a_w_s": nrm(ks[4], (N_A_LAYERS, GMLP_GROUPS, GMLP_BLOCK, GMLP_BLOCK), GMLP_BLOCK ** -0.5),
        "a_b_s": 1.0 + nrm(ks[5], (N_A_LAYERS, GMLP_GROUPS, GMLP_BLOCK), 0.01),
        "a_w_out": nrm(ks[6], (N_A_LAYERS, GMLP_WIDTH, D_MODEL), GMLP_WIDTH ** -0.5),
        "kv_norm_g": gain(ks[7], (D_MODEL,)),
        "w_kv": nrm(ks[8], (D_MODEL, 2 * HD), D_MODEL ** -0.5),
        "b_norm_g": gain(ks[9], (N_B_LAYERS, D_MODEL)),
        "b_w_q": nrm(ks[10], (N_B_LAYERS, D_MODEL, HD), D_MODEL ** -0.5),
        "b_rel_bias": nrm(ks[11], (N_B_LAYERS, N_HEADS, REL_SIZE), 0.5),
        "b_w_o": nrm(ks[12], (N_B_LAYERS, HD, D_MODEL), HD ** -0.5),
        "f_norm_g": gain(ks[13], (DEPTH, D_MODEL)),
        "f_w_in": nrm(ks[14], (DEPTH, D_MODEL, 2 * D_FF), D_MODEL ** -0.5),
        "f_conv_w": nrm(ks[15], (DEPTH, CONV_WIDTH, 2 * D_FF), CONV_WIDTH ** -0.5),
        "f_conv_b": nrm(ks[16], (DEPTH, 2 * D_FF), 0.01),
        "f_w_down": nrm(ks[17], (DEPTH, D_FF, D_MODEL), D_FF ** -0.5),
        "final_norm_g": gain(ks[18], (D_MODEL,)),
    }


def reference(x, a_norm_g, a_w_in, a_v_norm_g, a_w_s, a_b_s, a_w_out,
              kv_norm_g, w_kv, b_norm_g, b_w_q, b_rel_bias, b_w_o,
              f_norm_g, f_w_in, f_conv_w, f_conv_b, f_w_down, final_norm_g):
    B, S, _ = x.shape
    h = x
    k_shared = v_shared = None
    for l in range(DEPTH):
        if l < N_A_LAYERS:
            h = h + gmlp_mixer(rmsnorm(h, a_norm_g[l]), a_w_in[l], a_v_norm_g[l],
                               a_w_s[l], a_b_s[l], a_w_out[l])
        else:
            if l == N_A_LAYERS:
                kv = rmsnorm(h, kv_norm_g) @ w_kv
                k_shared, v_shared = jnp.split(kv, 2, axis=-1)
                k_shared = k_shared.reshape(B, S, N_HEADS, HEAD_DIM)
                v_shared = v_shared.reshape(B, S, N_HEADS, HEAD_DIM)
            j = l - N_A_LAYERS
            h = h + chunk_attention(rmsnorm(h, b_norm_g[j]), k_shared, v_shared,
                                    b_w_q[j], b_rel_bias[j], b_w_o[j])
        h = h + conv_ffn(rmsnorm(h, f_norm_g[l]), f_w_in[l], f_conv_w[l],
                         f_conv_b[l], f_w_down[l])
    return rmsnorm(h, final_norm_g)
```

```python
import functools

import jax
import jax.numpy as jnp
from jax import lax
from jax.experimental import pallas as pl
from jax.experimental.pallas import tpu as pltpu

CHUNK = 64
GMLP_BLOCK = 128
GMLP_GROUPS = 8
N_HEADS = 16
HEAD_DIM = 64
LEFT_CHUNKS = 8
PAD = LEFT_CHUNKS * CHUNK
REL_CLIP = 128
EPS = 1e-6
NEG_INF = -1e30

LANES = 128
SUBLANES = 8
MXU_DIM = 256
QPAIR = 2 * CHUNK
BAND = PAD + QPAIR
CONV_CARRY_ROWS = SUBLANES
VMEM_LIMIT_BYTES = 60 * 1024 * 1024

F32 = jnp.float32
BF16 = jnp.bfloat16


def _dot(a, b):
    return jnp.dot(a, b, preferred_element_type=F32)


def _rms_scale(x):
    return lax.rsqrt(jnp.mean(x * x, axis=-1, keepdims=True) + EPS)


def _gelu_tanh(x):
    c = 0.7978845608028654
    return 0.5 * x * (1.0 + jnp.tanh(c * (x + 0.044715 * (x * x * x))))


def _silu(x):
    return x * (1.0 / (1.0 + jnp.exp(-x)))


def _shift_rows(a, prev, k):
    rolled = pltpu.roll(a, k, 0)
    head = pltpu.roll(prev, k, 0)
    row = lax.broadcasted_iota(jnp.int32, prev.shape, 0)
    first = jnp.where(row < k, head, rolled[:CONV_CARRY_ROWS])
    return jnp.concatenate([first, rolled[CONV_CARRY_ROWS:]], axis=0)


def _conv_ffn_into(o_ref, fng_ref, fwin_ref, fcw_ref, fcb_ref, fwd_ref, carry_ref, tile):
    d_ff = fwd_ref.shape[0]
    h = o_ref[0]
    hn = (h * _rms_scale(h) * fng_ref[...]).astype(BF16)
    for c in range(d_ff // MXU_DIM):
        branches = []
        for half in range(2):
            col = half * d_ff + c * MXU_DIM
            cols = slice(col, col + MXU_DIM)
            a = _dot(hn, fwin_ref[:, cols])
            prev = carry_ref[:, cols]
            carry_ref[:, cols] = a[tile - CONV_CARRY_ROWS:, :]
            w = fcw_ref[:, cols]
            y = (w[0:1] * _shift_rows(a, prev, 2) + w[1:2] * _shift_rows(a, prev, 1)
                 + w[2:3] * a + fcb_ref[:, cols])
            branches.append(y)
        up, gate = branches
        act = (_silu(gate) * up).astype(BF16)
        o_ref[0] += _dot(act, fwd_ref[c * MXU_DIM:(c + 1) * MXU_DIM, :])


def _layer0_kernel(x_ref, ang_ref, awin_ref, avg_ref, ws_ref, bst_ref, awout_ref,
                   fng_ref, fwin_ref, fcw_ref, fcb_ref, fwd_ref,
                   o_ref, v_sc, gated_sc, carry_ref, *, tile):
    width = awout_ref.shape[0]

    @pl.when(pl.program_id(1) == 0)
    def _():
        carry_ref[...] = jnp.zeros_like(carry_ref)

    x = x_ref[0]
    hn = (x * _rms_scale(x) * ang_ref[...]).astype(BF16)

    ssq = jnp.zeros((tile, 1), F32)
    for j in range(width // MXU_DIM):
        cols = slice(j * MXU_DIM, (j + 1) * MXU_DIM)
        gv = _gelu_tanh(_dot(hn, awin_ref[:, width + j * MXU_DIM: width + (j + 1) * MXU_DIM]))
        v_sc[:, cols] = gv
        ssq = ssq + jnp.sum(gv * gv, axis=-1, keepdims=True)
    rv = lax.rsqrt(ssq / width + EPS)

    pi = lax.broadcasted_iota(jnp.int32, (GMLP_BLOCK, GMLP_BLOCK), 0) // CHUNK
    pj = lax.broadcasted_iota(jnp.int32, (GMLP_BLOCK, GMLP_BLOCK), 1) // CHUNK
    causal = pi >= pj

    groups_per_dot = MXU_DIM // LANES
    for gp in range(GMLP_GROUPS // groups_per_dot):
        cols = slice(gp * MXU_DIM, (gp + 1) * MXU_DIM)
        u = _gelu_tanh(_dot(hn, awin_ref[:, cols]))
        vn = (v_sc[:, cols] * rv * avg_ref[:, cols]).astype(BF16)
        for gg in range(groups_per_dot):
            g = gp * groups_per_dot + gg
            gcols = slice(g * LANES, (g + 1) * LANES)
            lcols = slice(gg * LANES, (gg + 1) * LANES)
            wm = jnp.where(causal, ws_ref[g], 0.0).astype(BF16)
            for blk in range(tile // GMLP_BLOCK):
                rows = slice(blk * GMLP_BLOCK, (blk + 1) * GMLP_BLOCK)
                s = _dot(wm, vn[rows, lcols]) + bst_ref[:, gcols]
                gated_sc[rows, gcols] = (u[rows, lcols] * s).astype(BF16)

    o_ref[0] = x + _dot(gated_sc[...], awout_ref[...])
    _conv_ffn_into(o_ref, fng_ref, fwin_ref, fcw_ref, fcb_ref, fwd_ref, carry_ref, tile)


def _layer1_kernel(h_ref, kvg_ref, bng_ref, wk_ref, wv_ref, wq_ref, bias_ref, wo_ref,
                   fng_ref, fwin_ref, fcw_ref, fcb_ref, fwd_ref, fing_ref,
                   o_ref, k_sc, v_sc, q_sc, att_sc, carry_ref, *, tile):
    s_idx = pl.program_id(1)
    hd = wq_ref.shape[1]

    @pl.when(s_idx == 0)
    def _():
        carry_ref[...] = jnp.zeros_like(carry_ref)
        k_sc[0:PAD, :] = jnp.zeros((PAD, hd), BF16)
        v_sc[0:PAD, :] = jnp.zeros((PAD, hd), BF16)

    h = h_ref[0]
    hr = h * _rms_scale(h)
    kvn = (hr * kvg_ref[...]).astype(BF16)
    qn = (hr * bng_ref[...]).astype(BF16)
    base = pl.multiple_of(PAD + s_idx * tile, LANES)
    scale = HEAD_DIM ** -0.5
    for j in range(hd // MXU_DIM):
        cols = slice(j * MXU_DIM, (j + 1) * MXU_DIM)
        k_sc[pl.ds(base, tile), cols] = _dot(kvn, wk_ref[:, cols]).astype(BF16)
        v_sc[pl.ds(base, tile), cols] = _dot(kvn, wv_ref[:, cols]).astype(BF16)
        q_sc[:, cols] = (_dot(qn, wq_ref[:, cols]) * scale).astype(BF16)

    lane = lax.broadcasted_iota(jnp.int32, (QPAIR, LANES), 1)
    key_idx = lax.broadcasted_iota(jnp.int32, (1, BAND), 1)
    heads_per_tile = LANES // HEAD_DIM
    for p in range(tile // QPAIR):
        rows = slice(p * QPAIR, (p + 1) * QPAIR)
        row0 = s_idx * tile + p * QPAIR
        band0 = pl.multiple_of(row0, LANES)
        before_start = jnp.where(key_idx >= PAD - row0, 0.0, NEG_INF)
        for hp in range(N_HEADS // heads_per_tile):
            hcols = slice(hp * LANES, (hp + 1) * LANES)
            qp = q_sc[rows, hcols]
            kp = k_sc[pl.ds(band0, BAND), hcols]
            vp = v_sc[pl.ds(band0, BAND), hcols]
            outs = []
            for hh in range(heads_per_tile):
                mine = (lane // HEAD_DIM) == hh
                qm = jnp.where(mine, qp, jnp.zeros_like(qp))
                sc = lax.dot_general(qm, kp, (((1,), (1,)), ((), ())),
                                     preferred_element_type=F32)
                sc = sc + bias_ref[hp * heads_per_tile + hh] + before_start
                m = jnp.max(sc, axis=-1, keepdims=True)
                e = jnp.exp(sc - m)
                denom = jnp.sum(e, axis=-1, keepdims=True)
                o = _dot(e.astype(BF16), vp) * (1.0 / denom)
                outs.append(o)
            merged = jnp.where((lane // HEAD_DIM) == 0, outs[0], outs[1])
            att_sc[rows, hcols] = merged.astype(BF16)

    o_ref[0] = h + _dot(att_sc[...], wo_ref[...])
    _conv_ffn_into(o_ref, fng_ref, fwin_ref, fcw_ref, fcb_ref, fwd_ref, carry_ref, tile)
    y = o_ref[0]
    o_ref[0] = y * _rms_scale(y) * fing_ref[...]


def _resident(arr):
    nd = arr.ndim
    return pl.BlockSpec(arr.shape, lambda b, s: (0,) * nd, pipeline_mode=pl.Buffered(1))


def _rel_bias_table(rel_bias):
    r = jnp.arange(QPAIR)[:, None]
    j = jnp.arange(BAND)[None, :]
    idx = jnp.clip(r + PAD - j, -REL_CLIP, REL_CLIP) + REL_CLIP
    lo = (r // CHUNK) * CHUNK
    inside = (j >= lo) & (j < lo + PAD + CHUNK)
    return jnp.where(inside[None], rel_bias[:, idx].astype(F32), NEG_INF)


def _pick_tile(seq):
    for t in (256, 128):
        if seq % t == 0:
            return t
    raise ValueError(f"sequence length {seq} must be a multiple of {QPAIR}")


def kernel(x, a_norm_g, a_w_in, a_v_norm_g, a_w_s, a_b_s, a_w_out, kv_norm_g, w_kv, b_norm_g, b_w_q, b_rel_bias, b_w_o, f_norm_g, f_w_in, f_conv_w, f_conv_b, f_w_down, final_norm_g):
    B, S, D = x.shape
    assert a_norm_g.shape[0] == 1 and b_norm_g.shape[0] == 1 and f_norm_g.shape[0] == 2
    width = a_w_out.shape[1]
    hd = b_w_q.shape[2]
    d_ff = f_w_down.shape[1]
    assert hd == N_HEADS * HEAD_DIM and width == GMLP_GROUPS * LANES and d_ff % MXU_DIM == 0
    tile = _pick_tile(S)
    grid = (B, S // tile)
    row = lambda v: v.reshape(1, -1).astype(F32)
    act_spec = pl.BlockSpec((1, tile, D), lambda b, s: (b, s, 0))
    params = pltpu.CompilerParams(dimension_semantics=("arbitrary", "arbitrary"),
                                  vmem_limit_bytes=VMEM_LIMIT_BYTES)

    def ffn_args(l):
        return (row(f_norm_g[l]), f_w_in[l].astype(BF16), f_conv_w[l].astype(F32),
                row(f_conv_b[l]), f_w_down[l].astype(BF16))

    bias_rows = jnp.repeat(a_b_s[0].T.astype(F32), LANES, axis=1)
    args0 = (row(a_norm_g[0]), a_w_in[0].astype(BF16), row(a_v_norm_g[0]), a_w_s[0].astype(F32),
             bias_rows, a_w_out[0].astype(BF16)) + ffn_args(0)
    h1 = pl.pallas_call(
        functools.partial(_layer0_kernel, tile=tile),
        grid=grid,
        in_specs=[act_spec] + [_resident(a) for a in args0],
        out_specs=act_spec,
        out_shape=jax.ShapeDtypeStruct((B, S, D), F32),
        scratch_shapes=[pltpu.VMEM((tile, width), F32),
                        pltpu.VMEM((tile, width), BF16),
                        pltpu.VMEM((CONV_CARRY_ROWS, 2 * d_ff), F32)],
        compiler_params=params,
        name="layer0_gmlp_convffn",
    )(x, *args0)

    args1 = (row(kv_norm_g), row(b_norm_g[0]), w_kv[:, :hd].astype(BF16), w_kv[:, hd:].astype(BF16),
             b_w_q[0].astype(BF16), _rel_bias_table(b_rel_bias[0]), b_w_o[0].astype(BF16)) \
        + ffn_args(1) + (row(final_norm_g),)
    out = pl.pallas_call(
        functools.partial(_layer1_kernel, tile=tile),
        grid=grid,
        in_specs=[act_spec] + [_resident(a) for a in args1],
        out_specs=act_spec,
        out_shape=jax.ShapeDtypeStruct((B, S, D), F32),
        scratch_shapes=[pltpu.VMEM((PAD + S, hd), BF16),
                        pltpu.VMEM((PAD + S, hd), BF16),
                        pltpu.VMEM((tile, hd), BF16),
                        pltpu.VMEM((tile, hd), BF16),
                        pltpu.VMEM((CONV_CARRY_ROWS, 2 * d_ff), F32)],
        compiler_params=params,
        name="layer1_attn_convffn",
    )(h1, *args1)
    return out.astype(x.dtype)
```

```python
import functools

import jax
import jax.numpy as jnp
from jax import lax
from jax.experimental import pallas as pl
from jax.experimental.pallas import tpu as pltpu

CHUNK = 64
GMLP_BLOCK = 128
GMLP_GROUPS = 8
N_HEADS = 16
HEAD_DIM = 64
LEFT_CHUNKS = 8
PAD = LEFT_CHUNKS * CHUNK
REL_CLIP = 128
EPS = 1e-6
NEG_INF = -1e30

LANES = 128
SUBLANES = 8
MXU_DIM = 256
QPAIR = 2 * CHUNK
BAND = PAD + QPAIR
CONV_CARRY_ROWS = SUBLANES
VMEM_LIMIT_BYTES = 60 * 1024 * 1024

F32 = jnp.float32
BF16 = jnp.bfloat16


def _dot(a, b):
    return jnp.dot(a, b, preferred_element_type=F32)


def _rms_scale(x):
    ssq = jnp.sum(_fold_lane_tiles(x * x, jnp.add), axis=-1, keepdims=True)
    return lax.rsqrt(ssq / x.shape[-1] + EPS)


def _gelu_tanh(x):
    c = 0.7978845608028654
    return 0.5 * x * (1.0 + jnp.tanh(c * (x + 0.044715 * (x * x * x))))


def _silu(x):
    return x * (1.0 / (1.0 + jnp.exp(-x)))


def _fold_lane_tiles(x, op):
    parts = [x[:, k * LANES:(k + 1) * LANES] for k in range(x.shape[1] // LANES)]
    while len(parts) > 1:
        parts = [op(parts[k], parts[k + 1]) if k + 1 < len(parts) else parts[k]
                 for k in range(0, len(parts), 2)]
    return parts[0]


def _shift_rows(a, prev, k):
    rolled = pltpu.roll(a, k, 0)
    head = pltpu.roll(prev, k, 0)
    row = lax.broadcasted_iota(jnp.int32, prev.shape, 0)
    first = jnp.where(row < k, head, rolled[:CONV_CARRY_ROWS])
    return jnp.concatenate([first, rolled[CONV_CARRY_ROWS:]], axis=0)


def _residual_proj_into(o_ref, h, act_ref, w_ref):
    act = act_ref[...]
    for j in range(w_ref.shape[1] // MXU_DIM):
        cols = slice(j * MXU_DIM, (j + 1) * MXU_DIM)
        o_ref[0, :, cols] = h[:, cols] + _dot(act, w_ref[:, cols])


def _conv_ffn_into(o_ref, fng_ref, fwin_ref, fcw_ref, fcb_ref, fwd_ref, carry_ref, tile):
    d_ff = fwd_ref.shape[0]
    h = o_ref[0]
    hn = (h * _rms_scale(h) * fng_ref[...]).astype(BF16)
    n_chunks = d_ff // MXU_DIM

    def up_proj(c):
        return [_dot(hn, fwin_ref[:, half * d_ff + c * MXU_DIM: half * d_ff + (c + 1) * MXU_DIM])
                for half in range(2)]

    pre = up_proj(0)
    for c in range(n_chunks):
        nxt = up_proj(c + 1) if c + 1 < n_chunks else None
        branches = []
        for half in range(2):
            col = half * d_ff + c * MXU_DIM
            cols = slice(col, col + MXU_DIM)
            a = pre[half]
            prev = carry_ref[:, cols]
            carry_ref[:, cols] = a[tile - CONV_CARRY_ROWS:, :]
            w = fcw_ref[:, cols]
            y = (w[0:1] * _shift_rows(a, prev, 2) + w[1:2] * _shift_rows(a, prev, 1)
                 + w[2:3] * a + fcb_ref[:, cols])
            branches.append(y)
        up, gate = branches
        act = (_silu(gate) * up).astype(BF16)
        o_ref[0] += _dot(act, fwd_ref[c * MXU_DIM:(c + 1) * MXU_DIM, :])
        pre = nxt


def _layer0_kernel(x_ref, ang_ref, awin_ref, avg_ref, ws_ref, bst_ref, awout_ref,
                   fng_ref, fwin_ref, fcw_ref, fcb_ref, fwd_ref,
                   o_ref, v_sc, gated_sc, carry_ref, *, tile):
    width = awout_ref.shape[0]

    @pl.when(pl.program_id(1) == 0)
    def _():
        carry_ref[...] = jnp.zeros_like(carry_ref)

    x = x_ref[0]
    hn = (x * _rms_scale(x) * ang_ref[...]).astype(BF16)

    ssq = jnp.zeros((tile, 1), F32)
    for j in range(width // MXU_DIM):
        cols = slice(j * MXU_DIM, (j + 1) * MXU_DIM)
        gv = _gelu_tanh(_dot(hn, awin_ref[:, width + j * MXU_DIM: width + (j + 1) * MXU_DIM]))
        v_sc[:, cols] = gv
        ssq = ssq + jnp.sum(_fold_lane_tiles(gv * gv, jnp.add), axis=-1, keepdims=True)
    rv = lax.rsqrt(ssq / width + EPS)

    pi = lax.broadcasted_iota(jnp.int32, (GMLP_BLOCK, GMLP_BLOCK), 0) // CHUNK
    pj = lax.broadcasted_iota(jnp.int32, (GMLP_BLOCK, GMLP_BLOCK), 1) // CHUNK
    causal = pi >= pj

    groups_per_dot = MXU_DIM // LANES
    for gp in range(GMLP_GROUPS // groups_per_dot):
        cols = slice(gp * MXU_DIM, (gp + 1) * MXU_DIM)
        u = _gelu_tanh(_dot(hn, awin_ref[:, cols]))
        vn = (v_sc[:, cols] * rv * avg_ref[:, cols]).astype(BF16)
        for gg in range(groups_per_dot):
            g = gp * groups_per_dot + gg
            gcols = slice(g * LANES, (g + 1) * LANES)
            lcols = slice(gg * LANES, (gg + 1) * LANES)
            wm = jnp.where(causal, ws_ref[g], 0.0).astype(BF16)
            for blk in range(tile // GMLP_BLOCK):
                rows = slice(blk * GMLP_BLOCK, (blk + 1) * GMLP_BLOCK)
                s = _dot(wm, vn[rows, lcols]) + bst_ref[:, gcols]
                gated_sc[rows, gcols] = (u[rows, lcols] * s).astype(BF16)

    _residual_proj_into(o_ref, x, gated_sc, awout_ref)
    _conv_ffn_into(o_ref, fng_ref, fwin_ref, fcw_ref, fcb_ref, fwd_ref, carry_ref, tile)


def _layer1_kernel(h_ref, kvg_ref, bng_ref, wk_ref, wv_ref, wq_ref, bias_ref, wo_ref,
                   fng_ref, fwin_ref, fcw_ref, fcb_ref, fwd_ref, fing_ref,
                   o_ref, k_sc, v_sc, q_sc, att_sc, carry_ref, *, tile):
    s_idx = pl.program_id(1)
    hd = wq_ref.shape[1]

    @pl.when(s_idx == 0)
    def _():
        carry_ref[...] = jnp.zeros_like(carry_ref)
        k_sc[0:PAD, :] = jnp.zeros((PAD, hd), BF16)
        v_sc[0:PAD, :] = jnp.zeros((PAD, hd), BF16)

    h = h_ref[0]
    hr = h * _rms_scale(h)
    kvn = (hr * kvg_ref[...]).astype(BF16)
    qn = (hr * bng_ref[...]).astype(BF16)
    base = pl.multiple_of(PAD + s_idx * tile, LANES)
    scale = HEAD_DIM ** -0.5
    for j in range(hd // MXU_DIM):
        cols = slice(j * MXU_DIM, (j + 1) * MXU_DIM)
        k_sc[pl.ds(base, tile), cols] = _dot(kvn, wk_ref[:, cols]).astype(BF16)
        v_sc[pl.ds(base, tile), cols] = _dot(kvn, wv_ref[:, cols]).astype(BF16)
        q_sc[:, cols] = (_dot(qn, wq_ref[:, cols]) * scale).astype(BF16)

    lane = lax.broadcasted_iota(jnp.int32, (QPAIR, LANES), 1)
    key_idx = lax.broadcasted_iota(jnp.int32, (1, BAND), 1)
    heads_per_tile = LANES // HEAD_DIM
    n_pairs = tile // QPAIR
    band0, before_start = [], []
    for p in range(n_pairs):
        row0 = s_idx * tile + p * QPAIR
        band0.append(pl.multiple_of(row0, LANES))
        before_start.append(jnp.where(key_idx >= PAD - row0, 0.0, NEG_INF))
    items = [(p, hp, hh) for p in range(n_pairs)
             for hp in range(N_HEADS // heads_per_tile) for hh in range(heads_per_tile)]

    def scores(i):
        p, hp, hh = items[i]
        hcols = slice(hp * LANES, (hp + 1) * LANES)
        qp = q_sc[p * QPAIR:(p + 1) * QPAIR, hcols]
        kp = k_sc[pl.ds(band0[p], BAND), hcols]
        qm = jnp.where((lane // HEAD_DIM) == hh, qp, jnp.zeros_like(qp))
        return lax.dot_general(qm, kp, (((1,), (1,)), ((), ())), preferred_element_type=F32)

    def biased_max(i, raw):
        p, hp, hh = items[i]
        sc = raw + bias_ref[hp * heads_per_tile + hh] + before_start[p]
        return sc, jnp.max(_fold_lane_tiles(sc, jnp.maximum), axis=-1, keepdims=True)

    def weighted_values(i, sc, m):
        p, hp, hh = items[i]
        e = jnp.exp(sc - m)
        denom = jnp.sum(_fold_lane_tiles(e, jnp.add), axis=-1, keepdims=True)
        vp = v_sc[pl.ds(band0[p], BAND), hp * LANES:(hp + 1) * LANES]
        return _dot(e.astype(BF16), vp), denom

    first_of_tile = {}

    def finish(i, o, denom):
        p, hp, hh = items[i]
        o = o * (1.0 / denom)
        if hh == 0:
            first_of_tile[(p, hp)] = o
        else:
            merged = jnp.where((lane // HEAD_DIM) == 0, first_of_tile.pop((p, hp)), o)
            att_sc[p * QPAIR:(p + 1) * QPAIR, hp * LANES:(hp + 1) * LANES] = merged.astype(BF16)

    raw, biased, weighted = {}, {}, {}
    for t in range(len(items) + 3):
        if t < len(items):
            raw[t] = scores(t)
        if 0 <= t - 1 < len(items):
            biased[t - 1] = biased_max(t - 1, raw.pop(t - 1))
        if 0 <= t - 2 < len(items):
            weighted[t - 2] = weighted_values(t - 2, *biased.pop(t - 2))
        if 0 <= t - 3 < len(items):
            finish(t - 3, *weighted.pop(t - 3))

    _residual_proj_into(o_ref, h, att_sc, wo_ref)
    _conv_ffn_into(o_ref, fng_ref, fwin_ref, fcw_ref, fcb_ref, fwd_ref, carry_ref, tile)
    y = o_ref[0]
    o_ref[0] = y * _rms_scale(y) * fing_ref[...]


def _resident(arr):
    nd = arr.ndim
    return pl.BlockSpec(arr.shape, lambda b, s: (0,) * nd, pipeline_mode=pl.Buffered(1))


def _rel_bias_table(rel_bias):
    n_heads = rel_bias.shape[0]
    rel_bias = rel_bias.astype(F32)
    far = jnp.broadcast_to(rel_bias[:, 2 * REL_CLIP:], (n_heads, BAND - 1 - REL_CLIP))
    near = rel_bias[:, REL_CLIP - QPAIR + 1:2 * REL_CLIP + 1][:, ::-1]
    by_dist = jnp.concatenate([far, near], axis=1)
    length = BAND + QPAIR - 1
    padded = jnp.pad(by_dist, ((0, 0), (0, 1)))
    skew = jnp.tile(padded, (1, QPAIR))[:, :QPAIR * length].reshape(n_heads, QPAIR, length)
    table = skew[:, :, QPAIR - 1:]
    r = jnp.arange(QPAIR)[:, None]
    j = jnp.arange(BAND)[None, :]
    lo = (r // CHUNK) * CHUNK
    inside = (j >= lo) & (j < lo + PAD + CHUNK)
    return jnp.where(inside[None], table, NEG_INF)


def _pick_tile(seq):
    for t in (256, 128):
        if seq % t == 0:
            return t
    raise ValueError(f"sequence length {seq} must be a multiple of {QPAIR}")


def kernel(x, a_norm_g, a_w_in, a_v_norm_g, a_w_s, a_b_s, a_w_out, kv_norm_g, w_kv, b_norm_g, b_w_q, b_rel_bias, b_w_o, f_norm_g, f_w_in, f_conv_w, f_conv_b, f_w_down, final_norm_g):
    B, S, D = x.shape
    assert a_norm_g.shape[0] == 1 and b_norm_g.shape[0] == 1 and f_norm_g.shape[0] == 2
    width = a_w_out.shape[1]
    hd = b_w_q.shape[2]
    d_ff = f_w_down.shape[1]
    assert hd == N_HEADS * HEAD_DIM and width == GMLP_GROUPS * LANES and d_ff % MXU_DIM == 0
    tile = _pick_tile(S)
    grid = (B, S // tile)
    row = lambda v: v.reshape(1, -1).astype(F32)
    act_spec = pl.BlockSpec((1, tile, D), lambda b, s: (b, s, 0))
    params = pltpu.CompilerParams(dimension_semantics=("arbitrary", "arbitrary"),
                                  vmem_limit_bytes=VMEM_LIMIT_BYTES)

    def ffn_args(l):
        return (row(f_norm_g[l]), f_w_in[l].astype(BF16), f_conv_w[l].astype(F32),
                row(f_conv_b[l]), f_w_down[l].astype(BF16))

    bias_rows = jnp.repeat(a_b_s[0].T.astype(F32), LANES, axis=1)
    args0 = (row(a_norm_g[0]), a_w_in[0].astype(BF16), row(a_v_norm_g[0]), a_w_s[0].astype(F32),
             bias_rows, a_w_out[0].astype(BF16)) + ffn_args(0)
    h1 = pl.pallas_call(
        functools.partial(_layer0_kernel, tile=tile),
        grid=grid,
        in_specs=[act_spec] + [_resident(a) for a in args0],
        out_specs=act_spec,
        out_shape=jax.ShapeDtypeStruct((B, S, D), F32),
        scratch_shapes=[pltpu.VMEM((tile, width), F32),
                        pltpu.VMEM((tile, width), BF16),
                        pltpu.VMEM((CONV_CARRY_ROWS, 2 * d_ff), F32)],
        compiler_params=params,
        name="layer0_gmlp_convffn",
    )(x, *args0)

    args1 = (row(kv_norm_g), row(b_norm_g[0]), w_kv[:, :hd].astype(BF16), w_kv[:, hd:].astype(BF16),
             b_w_q[0].astype(BF16), _rel_bias_table(b_rel_bias[0]), b_w_o[0].astype(BF16)) \
        + ffn_args(1) + (row(final_norm_g),)
    out = pl.pallas_call(
        functools.partial(_layer1_kernel, tile=tile),
        grid=grid,
        in_specs=[act_spec] + [_resident(a) for a in args1],
        out_specs=act_spec,
        out_shape=jax.ShapeDtypeStruct((B, S, D), F32),
        scratch_shapes=[pltpu.VMEM((PAD + S, hd), BF16),
                        pltpu.VMEM((PAD + S, hd), BF16),
                        pltpu.VMEM((tile, hd), BF16),
                        pltpu.VMEM((tile, hd), BF16),
                        pltpu.VMEM((CONV_CARRY_ROWS, 2 * d_ff), F32)],
        compiler_params=params,
        name="layer1_attn_convffn",
    )(h1, *args1)
    return out.astype(x.dtype)
```

```python
import functools

import jax
import jax.numpy as jnp
from jax import lax
from jax.experimental import pallas as pl
from jax.experimental.pallas import tpu as pltpu

CHUNK = 64
GMLP_BLOCK = 128
GMLP_GROUPS = 8
N_HEADS = 16
HEAD_DIM = 64
LEFT_CHUNKS = 8
PAD = LEFT_CHUNKS * CHUNK
REL_CLIP = 128
EPS = 1e-6
NEG_INF = -1e30

LANES = 128
SUBLANES = 8
MXU_DIM = 256
QPAIR = 2 * CHUNK
BAND = PAD + QPAIR
NEAR_COLS = QPAIR + REL_CLIP
FAR_COLS = BAND - NEAR_COLS
LOG2E = 1.4426950408889634
CONV_CARRY_ROWS = SUBLANES
VMEM_LIMIT_BYTES = 60 * 1024 * 1024
LAYER0_TILE = 256
LAYER1_TILE = 256

F32 = jnp.float32
BF16 = jnp.bfloat16


def _dot(a, b):
    return jnp.dot(a, b, preferred_element_type=F32)


def _rms_scale(x):
    ssq = jnp.sum(_fold_lane_tiles(x * x, jnp.add), axis=-1, keepdims=True)
    return lax.rsqrt(ssq / x.shape[-1] + EPS)


def _gelu_tanh(x):
    c = 0.7978845608028654
    return 0.5 * x * (1.0 + jnp.tanh(c * (x + 0.044715 * (x * x * x))))


def _silu(x):
    return x * (1.0 / (1.0 + jnp.exp(-x)))


def _fold_lane_tiles(x, op):
    parts = [x[:, k * LANES:(k + 1) * LANES] for k in range(x.shape[1] // LANES)]
    while len(parts) > 1:
        parts = [op(parts[k], parts[k + 1]) if k + 1 < len(parts) else parts[k]
                 for k in range(0, len(parts), 2)]
    return parts[0]


def _shift_rows(a, prev, k):
    rolled = pltpu.roll(a, k, 0)
    head = pltpu.roll(prev, k, 0)
    row = lax.broadcasted_iota(jnp.int32, prev.shape, 0)
    first = jnp.where(row < k, head, rolled[:CONV_CARRY_ROWS])
    return jnp.concatenate([first, rolled[CONV_CARRY_ROWS:]], axis=0)


def _residual_proj_into(o_ref, h, act_ref, w_ref):
    act = act_ref[...]
    for j in range(w_ref.shape[1] // MXU_DIM):
        cols = slice(j * MXU_DIM, (j + 1) * MXU_DIM)
        o_ref[0, :, cols] = h[:, cols] + _dot(act, w_ref[:, cols])


def _conv_ffn_into(o_ref, fng_ref, fwin_ref, fcw_ref, fcb_ref, fwd_ref, carry_ref, tile):
    d_ff = fwd_ref.shape[0]
    h = o_ref[0]
    hn = (h * _rms_scale(h) * fng_ref[...]).astype(BF16)
    n_chunks = d_ff // MXU_DIM

    def up_proj(c):
        return [_dot(hn, fwin_ref[:, half * d_ff + c * MXU_DIM: half * d_ff + (c + 1) * MXU_DIM])
                for half in range(2)]

    pre = up_proj(0)
    for c in range(n_chunks):
        nxt = up_proj(c + 1) if c + 1 < n_chunks else None
        branches = []
        for half in range(2):
            col = half * d_ff + c * MXU_DIM
            cols = slice(col, col + MXU_DIM)
            a = pre[half]
            prev = carry_ref[:, cols]
            carry_ref[:, cols] = a[tile - CONV_CARRY_ROWS:, :]
            w = fcw_ref[:, cols]
            y = (w[0:1] * _shift_rows(a, prev, 2) + w[1:2] * _shift_rows(a, prev, 1)
                 + w[2:3] * a + fcb_ref[:, cols])
            branches.append(y)
        up, gate = branches
        act = (_silu(gate) * up).astype(BF16)
        o_ref[0] += _dot(act, fwd_ref[c * MXU_DIM:(c + 1) * MXU_DIM, :])
        pre = nxt


def _layer0_kernel(x_ref, ang_ref, awin_ref, avg_ref, ws_ref, bst_ref, awout_ref,
                   fng_ref, fwin_ref, fcw_ref, fcb_ref, fwd_ref,
                   o_ref, v_sc, gated_sc, carry_ref, *, tile):
    width = awout_ref.shape[0]

    @pl.when(pl.program_id(1) == 0)
    def _():
        carry_ref[...] = jnp.zeros_like(carry_ref)

    x = x_ref[0]
    hn = (x * _rms_scale(x) * ang_ref[...]).astype(BF16)

    n_blocks = width // MXU_DIM

    def in_proj(col0):
        return _dot(hn, awin_ref[:, col0:col0 + MXU_DIM])

    ssq = jnp.zeros((tile, 1), F32)
    pre = in_proj(width)
    for j in range(n_blocks):
        nxt = in_proj(width + (j + 1) * MXU_DIM) if j + 1 < n_blocks else in_proj(0)
        gv = _gelu_tanh(pre)
        v_sc[:, j * MXU_DIM:(j + 1) * MXU_DIM] = gv
        ssq = ssq + jnp.sum(_fold_lane_tiles(gv * gv, jnp.add), axis=-1, keepdims=True)
        pre = nxt
    rv = lax.rsqrt(ssq / width + EPS)

    pi = lax.broadcasted_iota(jnp.int32, (GMLP_BLOCK, GMLP_BLOCK), 0) // CHUNK
    pj = lax.broadcasted_iota(jnp.int32, (GMLP_BLOCK, GMLP_BLOCK), 1) // CHUNK
    causal = pi >= pj

    groups_per_dot = MXU_DIM // LANES
    for gp in range(n_blocks):
        cols = slice(gp * MXU_DIM, (gp + 1) * MXU_DIM)
        nxt = in_proj((gp + 1) * MXU_DIM) if gp + 1 < n_blocks else None
        vn = (v_sc[:, cols] * rv * avg_ref[:, cols]).astype(BF16)
        mixed = []
        for gg in range(groups_per_dot):
            g = gp * groups_per_dot + gg
            wm = jnp.where(causal, ws_ref[g], 0.0).astype(BF16)
            mixed.append([_dot(wm, vn[blk * GMLP_BLOCK:(blk + 1) * GMLP_BLOCK, gg * LANES:(gg + 1) * LANES])
                          for blk in range(tile // GMLP_BLOCK)])
        u = _gelu_tanh(pre)
        for gg in range(groups_per_dot):
            g = gp * groups_per_dot + gg
            gcols = slice(g * LANES, (g + 1) * LANES)
            for blk in range(tile // GMLP_BLOCK):
                rows = slice(blk * GMLP_BLOCK, (blk + 1) * GMLP_BLOCK)
                s = mixed[gg][blk] + bst_ref[:, gcols]
                gated_sc[rows, gcols] = (u[rows, gg * LANES:(gg + 1) * LANES] * s).astype(BF16)
        pre = nxt

    _residual_proj_into(o_ref, x, gated_sc, awout_ref)
    _conv_ffn_into(o_ref, fng_ref, fwin_ref, fcw_ref, fcb_ref, fwd_ref, carry_ref, tile)


def _layer1_kernel(h_ref, kvg_ref, bng_ref, wk_ref, wv_ref, wq_ref, bias_ref, wo_ref,
                   fng_ref, fwin_ref, fcw_ref, fcb_ref, fwd_ref, fing_ref,
                   o_ref, k_sc, v_sc, q_sc, att_sc, carry_ref, *, tile):
    s_idx = pl.program_id(1)
    hd = wq_ref.shape[1]

    @pl.when(s_idx == 0)
    def _():
        carry_ref[...] = jnp.zeros_like(carry_ref)
        k_sc[0:PAD, :] = jnp.zeros((PAD, hd), BF16)
        v_sc[0:PAD, :] = jnp.zeros((PAD, hd), BF16)

    h = h_ref[0]
    hr = h * _rms_scale(h)
    kvn = (hr * kvg_ref[...]).astype(BF16)
    qn = (hr * bng_ref[...]).astype(BF16)
    base = pl.multiple_of(PAD + s_idx * tile, LANES)
    scale = HEAD_DIM ** -0.5 * LOG2E
    for j in range(hd // MXU_DIM):
        cols = slice(j * MXU_DIM, (j + 1) * MXU_DIM)
        k_sc[pl.ds(base, tile), cols] = _dot(kvn, wk_ref[:, cols]).astype(BF16)
        v_sc[pl.ds(base, tile), cols] = _dot(kvn, wv_ref[:, cols]).astype(BF16)
        q_sc[:, cols] = (_dot(qn, wq_ref[:, cols]) * scale).astype(BF16)

    lane = lax.broadcasted_iota(jnp.int32, (QPAIR, LANES), 1)
    key_idx = lax.broadcasted_iota(jnp.int32, (1, BAND), 1)
    heads_per_tile = LANES // HEAD_DIM
    n_pairs = tile // QPAIR
    row_in_pair = lax.broadcasted_iota(jnp.int32, (QPAIR, LANES), 0)
    oldest_hidden = jnp.where((row_in_pair >= CHUNK) & (lane < CHUNK), NEG_INF, 0.0)
    band0, before_start, oldest_mask = [], [], []
    for p in range(n_pairs):
        row0 = s_idx * tile + p * QPAIR
        band0.append(pl.multiple_of(row0, LANES))
        before_start.append(jnp.where(key_idx >= PAD - row0, 0.0, NEG_INF))
        m0 = oldest_hidden + before_start[p][:, :LANES]
        oldest_mask.append(jnp.concatenate([m0] * heads_per_tile, axis=0))
    items = [(p, hp) for p in range(n_pairs) for hp in range(N_HEADS // heads_per_tile)]

    def scores(i):
        p, hp = items[i]
        hcols = slice(hp * LANES, (hp + 1) * LANES)
        qp = q_sc[p * QPAIR:(p + 1) * QPAIR, hcols]
        kp = k_sc[pl.ds(band0[p], BAND), hcols]
        zero = jnp.zeros_like(qp)
        q2 = jnp.concatenate([jnp.where((lane // HEAD_DIM) == hh, qp, zero)
                              for hh in range(heads_per_tile)], axis=0)
        return lax.dot_general(q2, kp, (((1,), (1,)), ((), ())), preferred_element_type=F32)

    def biased_max(i, raw):
        p, hp = items[i]
        bs = before_start[p]
        near = bias_ref[hp]
        mid = FAR_COLS + NEAR_COLS - LANES
        sc = jnp.concatenate([raw[:, :LANES] + oldest_mask[p],
                              raw[:, LANES:FAR_COLS] + bs[:, LANES:FAR_COLS],
                              raw[:, FAR_COLS:mid] + near[:, :mid - FAR_COLS] + bs[:, FAR_COLS:mid],
                              raw[:, mid:] + near[:, mid - FAR_COLS:]], axis=1)
        return sc, jnp.max(_fold_lane_tiles(sc, jnp.maximum), axis=-1, keepdims=True)

    def weighted_values(i, sc, m):
        p, hp = items[i]
        e = jnp.exp2(sc - m)
        denom = jnp.sum(_fold_lane_tiles(e, jnp.add), axis=-1, keepdims=True)
        vp = v_sc[pl.ds(band0[p], BAND), hp * LANES:(hp + 1) * LANES]
        return _dot(e.astype(BF16), vp), denom

    def finish(i, o, denom):
        p, hp = items[i]
        o = o * (1.0 / denom)
        merged = jnp.where((lane // HEAD_DIM) == 0, o[:QPAIR], o[QPAIR:])
        att_sc[p * QPAIR:(p + 1) * QPAIR, hp * LANES:(hp + 1) * LANES] = merged.astype(BF16)

    raw, biased, weighted = {}, {}, {}
    for t in range(len(items) + 3):
        if t < len(items):
            raw[t] = scores(t)
        if 0 <= t - 1 < len(items):
            biased[t - 1] = biased_max(t - 1, raw.pop(t - 1))
        if 0 <= t - 2 < len(items):
            weighted[t - 2] = weighted_values(t - 2, *biased.pop(t - 2))
        if 0 <= t - 3 < len(items):
            finish(t - 3, *weighted.pop(t - 3))

    _residual_proj_into(o_ref, h, att_sc, wo_ref)
    _conv_ffn_into(o_ref, fng_ref, fwin_ref, fcw_ref, fcb_ref, fwd_ref, carry_ref, tile)
    y = o_ref[0]
    o_ref[0] = y * _rms_scale(y) * fing_ref[...]


def _resident(arr):
    nd = arr.ndim
    return pl.BlockSpec(arr.shape, lambda b, s: (0,) * nd, pipeline_mode=pl.Buffered(1))


def _rel_bias_table(rel_bias):
    n_heads = rel_bias.shape[0]
    rel_bias = rel_bias.astype(F32)
    far = jnp.broadcast_to(rel_bias[:, 2 * REL_CLIP:], (n_heads, BAND - 1 - REL_CLIP))
    near = rel_bias[:, REL_CLIP - QPAIR + 1:2 * REL_CLIP + 1][:, ::-1]
    by_dist = jnp.concatenate([far, near], axis=1)
    length = BAND + QPAIR - 1
    padded = jnp.pad(by_dist, ((0, 0), (0, 1)))
    skew = jnp.tile(padded, (1, QPAIR))[:, :QPAIR * length].reshape(n_heads, QPAIR, length)
    table = skew[:, :, QPAIR - 1:]
    r = jnp.arange(QPAIR)[:, None]
    j = jnp.arange(BAND)[None, :]
    lo = (r // CHUNK) * CHUNK
    inside = (j >= lo) & (j < lo + PAD + CHUNK)
    return jnp.where(inside[None], table, NEG_INF)


def _pick_tile(seq, largest):
    for t in range(largest, 0, -QPAIR):
        if seq % t == 0:
            return t
    raise ValueError(f"sequence length {seq} must be a multiple of {QPAIR}")


def kernel(x, a_norm_g, a_w_in, a_v_norm_g, a_w_s, a_b_s, a_w_out, kv_norm_g, w_kv, b_norm_g, b_w_q, b_rel_bias, b_w_o, f_norm_g, f_w_in, f_conv_w, f_conv_b, f_w_down, final_norm_g):
    B, S, D = x.shape
    assert a_norm_g.shape[0] == 1 and b_norm_g.shape[0] == 1 and f_norm_g.shape[0] == 2
    width = a_w_out.shape[1]
    hd = b_w_q.shape[2]
    d_ff = f_w_down.shape[1]
    assert hd == N_HEADS * HEAD_DIM and width == GMLP_GROUPS * LANES and d_ff % MXU_DIM == 0
    tile0 = _pick_tile(S, LAYER0_TILE)
    tile1 = _pick_tile(S, LAYER1_TILE)
    row = lambda v: v.reshape(1, -1).astype(F32)
    act_spec = lambda tile: pl.BlockSpec((1, tile, D), lambda b, s: (b, s, 0))
    params = pltpu.CompilerParams(dimension_semantics=("arbitrary", "arbitrary"),
                                  vmem_limit_bytes=VMEM_LIMIT_BYTES)

    def ffn_args(l):
        return (row(f_norm_g[l]), f_w_in[l].astype(BF16), f_conv_w[l].astype(F32),
                row(f_conv_b[l]), f_w_down[l].astype(BF16))

    bias_rows = jnp.repeat(a_b_s[0].T.astype(F32), LANES, axis=1)
    args0 = (row(a_norm_g[0]), a_w_in[0].astype(BF16), row(a_v_norm_g[0]), a_w_s[0].astype(F32),
             bias_rows, a_w_out[0].astype(BF16)) + ffn_args(0)
    h1 = pl.pallas_call(
        functools.partial(_layer0_kernel, tile=tile0),
        grid=(B, S // tile0),
        in_specs=[act_spec(tile0)] + [_resident(a) for a in args0],
        out_specs=act_spec(tile0),
        out_shape=jax.ShapeDtypeStruct((B, S, D), F32),
        scratch_shapes=[pltpu.VMEM((tile0, width), F32),
                        pltpu.VMEM((tile0, width), BF16),
                        pltpu.VMEM((CONV_CARRY_ROWS, 2 * d_ff), F32)],
        compiler_params=params,
        name="layer0_gmlp_convffn",
    )(x, *args0)

    rel = b_rel_bias[0].astype(F32)
    near_bias = _rel_bias_table((rel - rel[:, 2 * REL_CLIP:]) * LOG2E)[:, :, FAR_COLS:]
    args1 = (row(kv_norm_g), row(b_norm_g[0]), w_kv[:, :hd].astype(BF16), w_kv[:, hd:].astype(BF16),
             b_w_q[0].astype(BF16), near_bias.reshape(-1, 2 * QPAIR, NEAR_COLS),
             b_w_o[0].astype(BF16)) \
        + ffn_args(1) + (row(final_norm_g),)
    out = pl.pallas_call(
        functools.partial(_layer1_kernel, tile=tile1),
        grid=(B, S // tile1),
        in_specs=[act_spec(tile1)] + [_resident(a) for a in args1],
        out_specs=act_spec(tile1),
        out_shape=jax.ShapeDtypeStruct((B, S, D), F32),
        scratch_shapes=[pltpu.VMEM((PAD + S, hd), BF16),
                        pltpu.VMEM((PAD + S, hd), BF16),
                        pltpu.VMEM((tile1, hd), BF16),
                        pltpu.VMEM((tile1, hd), BF16),
                        pltpu.VMEM((CONV_CARRY_ROWS, 2 * d_ff), F32)],
        compiler_params=params,
        name="layer1_attn_convffn",
    )(h1, *args1)
    return out.astype(x.dtype)
```

```python
import functools

import jax
import jax.numpy as jnp
from jax import lax
from jax.experimental import pallas as pl
from jax.experimental.pallas import tpu as pltpu

CHUNK = 64
GMLP_BLOCK = 128
GMLP_GROUPS = 8
N_HEADS = 16
HEAD_DIM = 64
LEFT_CHUNKS = 8
PAD = LEFT_CHUNKS * CHUNK
REL_CLIP = 128
EPS = 1e-6
NEG_INF = -1e30

LANES = 128
SUBLANES = 8
MXU_DIM = 256
QPAIR = 2 * CHUNK
BAND = PAD + QPAIR
NEAR_COLS = QPAIR + REL_CLIP
FAR_COLS = BAND - NEAR_COLS
LOG2E = 1.4426950408889634
CONV_CARRY_ROWS = SUBLANES
VMEM_LIMIT_BYTES = 60 * 1024 * 1024
PASS_ROWS = 256
PASSES_PER_STEP = 2

F32 = jnp.float32
BF16 = jnp.bfloat16


def _dot(a, b):
    return jnp.dot(a, b, preferred_element_type=F32)


def _rms_scale(x):
    ssq = jnp.sum(_fold_lane_tiles(x * x, jnp.add), axis=-1, keepdims=True)
    return lax.rsqrt(ssq / x.shape[-1] + EPS)


def _gelu_tanh(x):
    c = 0.7978845608028654
    return 0.5 * x * (1.0 + jnp.tanh(c * (x + 0.044715 * (x * x * x))))


def _silu(x):
    return x * (1.0 / (1.0 + jnp.exp2(x * -LOG2E)))


def _fold_lane_tiles(x, op):
    parts = [x[:, k * LANES:(k + 1) * LANES] for k in range(x.shape[1] // LANES)]
    while len(parts) > 1:
        parts = [op(parts[k], parts[k + 1]) if k + 1 < len(parts) else parts[k]
                 for k in range(0, len(parts), 2)]
    return parts[0]


def _shift_rows(a, prev, k):
    rolled = pltpu.roll(a, k, 0)
    head = pltpu.roll(prev, k, 0)
    row = lax.broadcasted_iota(jnp.int32, prev.shape, 0)
    first = jnp.where(row < k, head, rolled[:CONV_CARRY_ROWS])
    return jnp.concatenate([first, rolled[CONV_CARRY_ROWS:]], axis=0)


def _residual_proj_into(o_rows, h, act_ref, w_ref):
    act = act_ref[...]
    for j in range(w_ref.shape[1] // MXU_DIM):
        cols = slice(j * MXU_DIM, (j + 1) * MXU_DIM)
        o_rows[:, cols] = h[:, cols] + _dot(act, w_ref[:, cols])


def _conv_ffn_into(o_rows, fng_ref, fwin_ref, fcw_ref, fcb_ref, fwd_ref, carry_ref):
    d_ff = fwd_ref.shape[0]
    rows = o_rows.shape[0]
    h = o_rows[...]
    hn = (h * _rms_scale(h) * fng_ref[...]).astype(BF16)
    n_chunks = d_ff // MXU_DIM

    def up_proj(c):
        return [_dot(hn, fwin_ref[:, half * d_ff + c * MXU_DIM: half * d_ff + (c + 1) * MXU_DIM])
                for half in range(2)]

    pre = up_proj(0)
    for c in range(n_chunks):
        nxt = up_proj(c + 1) if c + 1 < n_chunks else None
        branches = []
        for half in range(2):
            col = half * d_ff + c * MXU_DIM
            cols = slice(col, col + MXU_DIM)
            a = pre[half]
            prev = carry_ref[:, cols]
            carry_ref[:, cols] = a[rows - CONV_CARRY_ROWS:, :]
            w = fcw_ref[:, cols]
            y = (w[0:1] * _shift_rows(a, prev, 2) + w[1:2] * _shift_rows(a, prev, 1)
                 + w[2:3] * a + fcb_ref[:, cols])
            branches.append(y)
        up, gate = branches
        act = (_silu(gate) * up).astype(BF16)
        o_rows[...] += _dot(act, fwd_ref[c * MXU_DIM:(c + 1) * MXU_DIM, :])
        pre = nxt


def _layer0_rows(x_rows, o_rows, ang_ref, awin_ref, avg_ref, ws_ref, bst_ref, awout_ref,
                 fng_ref, fwin_ref, fcw_ref, fcb_ref, fwd_ref, v_sc, gated_sc, carry_ref):
    rows = x_rows.shape[0]
    width = awout_ref.shape[0]
    x = x_rows[...]
    hn = (x * _rms_scale(x) * ang_ref[...]).astype(BF16)

    n_blocks = width // MXU_DIM

    def in_proj(col0):
        return _dot(hn, awin_ref[:, col0:col0 + MXU_DIM])

    ssq = jnp.zeros((rows, 1), F32)
    pre = in_proj(width)
    for j in range(n_blocks):
        nxt = in_proj(width + (j + 1) * MXU_DIM) if j + 1 < n_blocks else in_proj(0)
        gv = _gelu_tanh(pre)
        v_sc[:, j * MXU_DIM:(j + 1) * MXU_DIM] = gv
        ssq = ssq + jnp.sum(_fold_lane_tiles(gv * gv, jnp.add), axis=-1, keepdims=True)
        pre = nxt
    rv = lax.rsqrt(ssq / width + EPS)

    pi = lax.broadcasted_iota(jnp.int32, (GMLP_BLOCK, GMLP_BLOCK), 0) // CHUNK
    pj = lax.broadcasted_iota(jnp.int32, (GMLP_BLOCK, GMLP_BLOCK), 1) // CHUNK
    causal = pi >= pj

    groups_per_dot = MXU_DIM // LANES
    for gp in range(n_blocks):
        cols = slice(gp * MXU_DIM, (gp + 1) * MXU_DIM)
        nxt = in_proj((gp + 1) * MXU_DIM) if gp + 1 < n_blocks else None
        vn = (v_sc[:, cols] * rv * avg_ref[:, cols]).astype(BF16)
        mixed = []
        for gg in range(groups_per_dot):
            g = gp * groups_per_dot + gg
            wm = jnp.where(causal, ws_ref[g], 0.0).astype(BF16)
            mixed.append([_dot(wm, vn[blk * GMLP_BLOCK:(blk + 1) * GMLP_BLOCK, gg * LANES:(gg + 1) * LANES])
                          for blk in range(rows // GMLP_BLOCK)])
        u = _gelu_tanh(pre)
        for gg in range(groups_per_dot):
            g = gp * groups_per_dot + gg
            gcols = slice(g * LANES, (g + 1) * LANES)
            for blk in range(rows // GMLP_BLOCK):
                brows = slice(blk * GMLP_BLOCK, (blk + 1) * GMLP_BLOCK)
                s = mixed[gg][blk] + bst_ref[:, gcols]
                gated_sc[brows, gcols] = (u[brows, gg * LANES:(gg + 1) * LANES] * s).astype(BF16)
        pre = nxt

    _residual_proj_into(o_rows, x, gated_sc, awout_ref)
    _conv_ffn_into(o_rows, fng_ref, fwin_ref, fcw_ref, fcb_ref, fwd_ref, carry_ref)


def _layer0_kernel(x_ref, *refs, tile):
    params, o_ref, scratch = refs[:11], refs[11], refs[12:]
    carry_ref = scratch[-1]

    @pl.when(pl.program_id(1) == 0)
    def _():
        carry_ref[...] = jnp.zeros_like(carry_ref)

    for sub in range(tile // PASS_ROWS):
        rows = pl.ds(sub * PASS_ROWS, PASS_ROWS)
        _layer0_rows(x_ref.at[0, rows], o_ref.at[0, rows], *params, *scratch)


def _layer1_rows(pos0, h_rows, o_rows, kvg_ref, bng_ref, wk_ref, wv_ref, wq_ref, bias_ref, wo_ref,
                 fng_ref, fwin_ref, fcw_ref, fcb_ref, fwd_ref, fing_ref,
                 k_sc, v_sc, q_sc, att_sc, carry_ref):
    rows = h_rows.shape[0]
    hd = wq_ref.shape[1]
    h = h_rows[...]
    hr = h * _rms_scale(h)
    kvn = (hr * kvg_ref[...]).astype(BF16)
    qn = (hr * bng_ref[...]).astype(BF16)
    base = pl.multiple_of(PAD + pos0, LANES)
    scale = HEAD_DIM ** -0.5 * LOG2E
    for j in range(hd // MXU_DIM):
        cols = slice(j * MXU_DIM, (j + 1) * MXU_DIM)
        k_sc[pl.ds(base, rows), cols] = _dot(kvn, wk_ref[:, cols]).astype(BF16)
        v_sc[pl.ds(base, rows), cols] = _dot(kvn, wv_ref[:, cols]).astype(BF16)
        q_sc[:, cols] = (_dot(qn, wq_ref[:, cols]) * scale).astype(BF16)

    lane = lax.broadcasted_iota(jnp.int32, (QPAIR, LANES), 1)
    key_idx = lax.broadcasted_iota(jnp.int32, (1, BAND), 1)
    heads_per_tile = LANES // HEAD_DIM
    n_pairs = rows // QPAIR
    row_in_pair = lax.broadcasted_iota(jnp.int32, (QPAIR, LANES), 0)
    oldest_hidden = jnp.where((row_in_pair >= CHUNK) & (lane < CHUNK), NEG_INF, 0.0)
    band0, before_start, oldest_mask = [], [], []
    for p in range(n_pairs):
        row0 = pos0 + p * QPAIR
        band0.append(pl.multiple_of(row0, LANES))
        before_start.append(jnp.where(key_idx >= PAD - row0, 0.0, NEG_INF))
        m0 = oldest_hidden + before_start[p][:, :LANES]
        oldest_mask.append(jnp.concatenate([m0] * heads_per_tile, axis=0))
    items = [(p, hp) for p in range(n_pairs) for hp in range(N_HEADS // heads_per_tile)]

    def scores(i):
        p, hp = items[i]
        hcols = slice(hp * LANES, (hp + 1) * LANES)
        qp = q_sc[p * QPAIR:(p + 1) * QPAIR, hcols]
        kp = k_sc[pl.ds(band0[p], BAND), hcols]
        zero = jnp.zeros_like(qp)
        q2 = jnp.concatenate([jnp.where((lane // HEAD_DIM) == hh, qp, zero)
                              for hh in range(heads_per_tile)], axis=0)
        return lax.dot_general(q2, kp, (((1,), (1,)), ((), ())), preferred_element_type=F32)

    def biased_max(i, raw):
        p, hp = items[i]
        bs = before_start[p]
        near = bias_ref[hp]
        mid = FAR_COLS + NEAR_COLS - LANES
        sc = jnp.concatenate([raw[:, :LANES] + oldest_mask[p],
                              raw[:, LANES:FAR_COLS] + bs[:, LANES:FAR_COLS],
                              raw[:, FAR_COLS:mid] + near[:, :mid - FAR_COLS] + bs[:, FAR_COLS:mid],
                              raw[:, mid:] + near[:, mid - FAR_COLS:]], axis=1)
        return sc, jnp.max(_fold_lane_tiles(sc, jnp.maximum), axis=-1, keepdims=True)

    def weighted_values(i, sc, m):
        p, hp = items[i]
        e = jnp.exp2(sc - m)
        denom = jnp.sum(_fold_lane_tiles(e, jnp.add), axis=-1, keepdims=True)
        vp = v_sc[pl.ds(band0[p], BAND), hp * LANES:(hp + 1) * LANES]
        return _dot(e.astype(BF16), vp), denom

    def finish(i, o, denom):
        p, hp = items[i]
        o = o * (1.0 / denom)
        merged = jnp.where((lane // HEAD_DIM) == 0, o[:QPAIR], o[QPAIR:])
        att_sc[p * QPAIR:(p + 1) * QPAIR, hp * LANES:(hp + 1) * LANES] = merged.astype(BF16)

    raw, biased, weighted = {}, {}, {}
    for t in range(len(items) + 3):
        if t < len(items):
            raw[t] = scores(t)
        if 0 <= t - 1 < len(items):
            biased[t - 1] = biased_max(t - 1, raw.pop(t - 1))
        if 0 <= t - 2 < len(items):
            weighted[t - 2] = weighted_values(t - 2, *biased.pop(t - 2))
        if 0 <= t - 3 < len(items):
            finish(t - 3, *weighted.pop(t - 3))

    _residual_proj_into(o_rows, h, att_sc, wo_ref)
    _conv_ffn_into(o_rows, fng_ref, fwin_ref, fcw_ref, fcb_ref, fwd_ref, carry_ref)
    y = o_rows[...]
    o_rows[...] = y * _rms_scale(y) * fing_ref[...]


def _layer1_kernel(h_ref, *refs, tile):
    params, o_ref, scratch = refs[:13], refs[13], refs[14:]
    k_sc, v_sc, carry_ref = scratch[0], scratch[1], scratch[-1]
    s_idx = pl.program_id(1)

    @pl.when(s_idx == 0)
    def _():
        carry_ref[...] = jnp.zeros_like(carry_ref)
        k_sc[0:PAD, :] = jnp.zeros((PAD, k_sc.shape[1]), BF16)
        v_sc[0:PAD, :] = jnp.zeros((PAD, v_sc.shape[1]), BF16)

    for sub in range(tile // PASS_ROWS):
        rows = pl.ds(sub * PASS_ROWS, PASS_ROWS)
        _layer1_rows(s_idx * tile + sub * PASS_ROWS, h_ref.at[0, rows], o_ref.at[0, rows],
                     *params, *scratch)


def _resident(arr):
    nd = arr.ndim
    return pl.BlockSpec(arr.shape, lambda b, s: (0,) * nd, pipeline_mode=pl.Buffered(1))


def _rel_bias_table(rel_bias):
    n_heads = rel_bias.shape[0]
    rel_bias = rel_bias.astype(F32)
    far = jnp.broadcast_to(rel_bias[:, 2 * REL_CLIP:], (n_heads, BAND - 1 - REL_CLIP))
    near = rel_bias[:, REL_CLIP - QPAIR + 1:2 * REL_CLIP + 1][:, ::-1]
    by_dist = jnp.concatenate([far, near], axis=1)
    length = BAND + QPAIR - 1
    padded = jnp.pad(by_dist, ((0, 0), (0, 1)))
    skew = jnp.tile(padded, (1, QPAIR))[:, :QPAIR * length].reshape(n_heads, QPAIR, length)
    table = skew[:, :, QPAIR - 1:]
    r = jnp.arange(QPAIR)[:, None]
    j = jnp.arange(BAND)[None, :]
    lo = (r // CHUNK) * CHUNK
    inside = (j >= lo) & (j < lo + PAD + CHUNK)
    return jnp.where(inside[None], table, NEG_INF)


def _pick_tile(seq):
    for n in range(PASSES_PER_STEP, 0, -1):
        if seq % (n * PASS_ROWS) == 0:
            return n * PASS_ROWS
    raise ValueError(f"sequence length {seq} must be a multiple of {PASS_ROWS}")


def kernel(x, a_norm_g, a_w_in, a_v_norm_g, a_w_s, a_b_s, a_w_out, kv_norm_g, w_kv, b_norm_g, b_w_q, b_rel_bias, b_w_o, f_norm_g, f_w_in, f_conv_w, f_conv_b, f_w_down, final_norm_g):
    B, S, D = x.shape
    assert a_norm_g.shape[0] == 1 and b_norm_g.shape[0] == 1 and f_norm_g.shape[0] == 2
    width = a_w_out.shape[1]
    hd = b_w_q.shape[2]
    d_ff = f_w_down.shape[1]
    assert hd == N_HEADS * HEAD_DIM and width == GMLP_GROUPS * LANES and d_ff % MXU_DIM == 0
    tile = _pick_tile(S)
    grid = (B, S // tile)
    row = lambda v: v.reshape(1, -1).astype(F32)
    act_spec = pl.BlockSpec((1, tile, D), lambda b, s: (b, s, 0))
    params = pltpu.CompilerParams(dimension_semantics=("arbitrary", "arbitrary"),
                                  vmem_limit_bytes=VMEM_LIMIT_BYTES)

    def ffn_args(l):
        return (row(f_norm_g[l]), f_w_in[l].astype(BF16), f_conv_w[l].astype(F32),
                row(f_conv_b[l]), f_w_down[l].astype(BF16))

    bias_rows = jnp.repeat(a_b_s[0].T.astype(F32), LANES, axis=1)
    args0 = (row(a_norm_g[0]), a_w_in[0].astype(BF16), row(a_v_norm_g[0]), a_w_s[0].astype(F32),
             bias_rows, a_w_out[0].astype(BF16)) + ffn_args(0)
    h1 = pl.pallas_call(
        functools.partial(_layer0_kernel, tile=tile),
        grid=grid,
        in_specs=[act_spec] + [_resident(a) for a in args0],
        out_specs=act_spec,
        out_shape=jax.ShapeDtypeStruct((B, S, D), F32),
        scratch_shapes=[pltpu.VMEM((PASS_ROWS, width), F32),
                        pltpu.VMEM((PASS_ROWS, width), BF16),
                        pltpu.VMEM((CONV_CARRY_ROWS, 2 * d_ff), F32)],
        compiler_params=params,
        name="layer0_gmlp_convffn",
    )(x, *args0)

    rel = b_rel_bias[0].astype(F32)
    near_bias = _rel_bias_table((rel - rel[:, 2 * REL_CLIP:]) * LOG2E)[:, :, FAR_COLS:]
    args1 = (row(kv_norm_g), row(b_norm_g[0]), w_kv[:, :hd].astype(BF16), w_kv[:, hd:].astype(BF16),
             b_w_q[0].astype(BF16), near_bias.reshape(-1, 2 * QPAIR, NEAR_COLS),
             b_w_o[0].astype(BF16)) \
        + ffn_args(1) + (row(final_norm_g),)
    out = pl.pallas_call(
        functools.partial(_layer1_kernel, tile=tile),
        grid=grid,
        in_specs=[act_spec] + [_resident(a) for a in args1],
        out_specs=act_spec,
        out_shape=jax.ShapeDtypeStruct((B, S, D), F32),
        scratch_shapes=[pltpu.VMEM((PAD + S, hd), BF16),
                        pltpu.VMEM((PAD + S, hd), BF16),
                        pltpu.VMEM((PASS_ROWS, hd), BF16),
                        pltpu.VMEM((PASS_ROWS, hd), BF16),
                        pltpu.VMEM((CONV_CARRY_ROWS, 2 * d_ff), F32)],
        compiler_params=params,
        name="layer1_attn_convffn",
    )(h1, *args1)
    return out.astype(x.dtype)
```

```python
import functools

import jax
import jax.numpy as jnp
from jax import lax
from jax.experimental import pallas as pl
from jax.experimental.pallas import tpu as pltpu

CHUNK = 64
GMLP_BLOCK = 128
GMLP_GROUPS = 8
N_HEADS = 16
HEAD_DIM = 64
LEFT_CHUNKS = 8
PAD = LEFT_CHUNKS * CHUNK
REL_CLIP = 128
EPS = 1e-6
NEG_INF = -1e30

LANES = 128
SUBLANES = 8
MXU_DIM = 256
QPAIR = 2 * CHUNK
BAND = PAD + QPAIR
NEAR_COLS = QPAIR + REL_CLIP
FAR_COLS = BAND - NEAR_COLS
LOG2E = 1.4426950408889634
CONV_CARRY_ROWS = SUBLANES
VMEM_LIMIT_BYTES = 60 * 1024 * 1024
PASS_ROWS = 256
PASSES_PER_STEP = 2
CAST_BLOCK_BYTES = 3 * 1024 * 1024

F32 = jnp.float32
BF16 = jnp.bfloat16


def _dot(a, b):
    return jnp.dot(a, b, preferred_element_type=F32)


def _rms_scale(x):
    ssq = jnp.sum(_fold_lane_tiles(x * x, jnp.add), axis=-1, keepdims=True)
    return lax.rsqrt(ssq / x.shape[-1] + EPS)


def _gelu_tanh(x):
    c = 0.7978845608028654
    return 0.5 * x * (1.0 + jnp.tanh(c * (x + 0.044715 * (x * x * x))))


def _silu(x):
    return x * (1.0 / (1.0 + jnp.exp2(x * -LOG2E)))


def _fold_lane_tiles(x, op):
    parts = [x[:, k * LANES:(k + 1) * LANES] for k in range(x.shape[1] // LANES)]
    while len(parts) > 1:
        parts = [op(parts[k], parts[k + 1]) if k + 1 < len(parts) else parts[k]
                 for k in range(0, len(parts), 2)]
    return parts[0]


def _shift_rows(a, prev, k):
    rolled = pltpu.roll(a, k, 0)
    head = pltpu.roll(prev, k, 0)
    row = lax.broadcasted_iota(jnp.int32, prev.shape, 0)
    first = jnp.where(row < k, head, rolled[:CONV_CARRY_ROWS])
    return jnp.concatenate([first, rolled[CONV_CARRY_ROWS:]], axis=0)


def _residual_proj_into(o_rows, h, act_ref, w_ref):
    act = act_ref[...]
    for j in range(w_ref.shape[1] // MXU_DIM):
        cols = slice(j * MXU_DIM, (j + 1) * MXU_DIM)
        o_rows[:, cols] = h[:, cols] + _dot(act, w_ref[:, cols])


def _conv_ffn_into(o_rows, fng_ref, fwin_ref, fcw_ref, fcb_ref, fwd_ref, carry_ref):
    d_ff = fwd_ref.shape[0]
    rows = o_rows.shape[0]
    h = o_rows[...]
    hn = (h * _rms_scale(h) * fng_ref[...]).astype(BF16)
    n_chunks = d_ff // MXU_DIM

    def up_proj(c):
        return [_dot(hn, fwin_ref[:, half * d_ff + c * MXU_DIM: half * d_ff + (c + 1) * MXU_DIM])
                for half in range(2)]

    pre = up_proj(0)
    for c in range(n_chunks):
        nxt = up_proj(c + 1) if c + 1 < n_chunks else None
        branches = []
        for half in range(2):
            col = half * d_ff + c * MXU_DIM
            cols = slice(col, col + MXU_DIM)
            a = pre[half]
            prev = carry_ref[:, cols]
            carry_ref[:, cols] = a[rows - CONV_CARRY_ROWS:, :]
            w = fcw_ref[:, cols]
            y = (w[0:1] * _shift_rows(a, prev, 2) + w[1:2] * _shift_rows(a, prev, 1)
                 + w[2:3] * a + fcb_ref[:, cols])
            branches.append(y)
        up, gate = branches
        act = (_silu(gate) * up).astype(BF16)
        o_rows[...] += _dot(act, fwd_ref[c * MXU_DIM:(c + 1) * MXU_DIM, :])
        pre = nxt


def _layer0_rows(x_rows, o_rows, ang_ref, awin_ref, avg_ref, ws_ref, bst_ref, awout_ref,
                 fng_ref, fwin_ref, fcw_ref, fcb_ref, fwd_ref, v_sc, gated_sc, carry_ref):
    rows = x_rows.shape[0]
    width = awout_ref.shape[0]
    x = x_rows[...]
    hn = (x * _rms_scale(x) * ang_ref[...]).astype(BF16)

    n_blocks = width // MXU_DIM

    def in_proj(col0):
        return _dot(hn, awin_ref[:, col0:col0 + MXU_DIM])

    ssq = jnp.zeros((rows, 1), F32)
    pre = in_proj(width)
    for j in range(n_blocks):
        nxt = in_proj(width + (j + 1) * MXU_DIM) if j + 1 < n_blocks else in_proj(0)
        gv = _gelu_tanh(pre)
        v_sc[:, j * MXU_DIM:(j + 1) * MXU_DIM] = gv
        ssq = ssq + jnp.sum(_fold_lane_tiles(gv * gv, jnp.add), axis=-1, keepdims=True)
        pre = nxt
    rv = lax.rsqrt(ssq / width + EPS)

    pi = lax.broadcasted_iota(jnp.int32, (GMLP_BLOCK, GMLP_BLOCK), 0) // CHUNK
    pj = lax.broadcasted_iota(jnp.int32, (GMLP_BLOCK, GMLP_BLOCK), 1) // CHUNK
    causal = pi >= pj

    groups_per_dot = MXU_DIM // LANES
    for gp in range(n_blocks):
        cols = slice(gp * MXU_DIM, (gp + 1) * MXU_DIM)
        nxt = in_proj((gp + 1) * MXU_DIM) if gp + 1 < n_blocks else None
        vn = (v_sc[:, cols] * rv * avg_ref[:, cols]).astype(BF16)
        mixed = []
        for gg in range(groups_per_dot):
            g = gp * groups_per_dot + gg
            wm = jnp.where(causal, ws_ref[g], 0.0).astype(BF16)
            mixed.append([_dot(wm, vn[blk * GMLP_BLOCK:(blk + 1) * GMLP_BLOCK, gg * LANES:(gg + 1) * LANES])
                          for blk in range(rows // GMLP_BLOCK)])
        u = _gelu_tanh(pre)
        for gg in range(groups_per_dot):
            g = gp * groups_per_dot + gg
            gcols = slice(g * LANES, (g + 1) * LANES)
            for blk in range(rows // GMLP_BLOCK):
                brows = slice(blk * GMLP_BLOCK, (blk + 1) * GMLP_BLOCK)
                s = mixed[gg][blk] + bst_ref[:, gcols]
                gated_sc[brows, gcols] = (u[brows, gg * LANES:(gg + 1) * LANES] * s).astype(BF16)
        pre = nxt

    _residual_proj_into(o_rows, x, gated_sc, awout_ref)
    _conv_ffn_into(o_rows, fng_ref, fwin_ref, fcw_ref, fcb_ref, fwd_ref, carry_ref)


def _layer0_kernel(x_ref, *refs, tile):
    params, o_ref, scratch = refs[:11], refs[11], refs[12:]
    carry_ref = scratch[-1]

    @pl.when(pl.program_id(1) == 0)
    def _():
        carry_ref[...] = jnp.zeros_like(carry_ref)

    for sub in range(tile // PASS_ROWS):
        rows = pl.ds(sub * PASS_ROWS, PASS_ROWS)
        _layer0_rows(x_ref.at[0, rows], o_ref.at[0, rows], *params, *scratch)


def _layer1_rows(pos0, h_rows, o_rows, kvg_ref, bng_ref, wkv_ref, wq_ref, bias_ref, wo_ref,
                 fng_ref, fwin_ref, fcw_ref, fcb_ref, fwd_ref, fing_ref,
                 k_sc, v_sc, q_sc, att_sc, carry_ref):
    rows = h_rows.shape[0]
    hd = wq_ref.shape[1]
    h = h_rows[...]
    hr = h * _rms_scale(h)
    kvn = (hr * kvg_ref[...]).astype(BF16)
    qn = (hr * bng_ref[...]).astype(BF16)
    base = pl.multiple_of(PAD + pos0, LANES)
    scale = HEAD_DIM ** -0.5 * LOG2E
    for j in range(hd // MXU_DIM):
        cols = slice(j * MXU_DIM, (j + 1) * MXU_DIM)
        k_sc[pl.ds(base, rows), cols] = _dot(kvn, wkv_ref[:, cols]).astype(BF16)
        v_sc[pl.ds(base, rows), cols] = _dot(kvn, wkv_ref[:, hd + j * MXU_DIM:hd + (j + 1) * MXU_DIM]).astype(BF16)
        q_sc[:, cols] = (_dot(qn, wq_ref[:, cols]) * scale).astype(BF16)

    lane = lax.broadcasted_iota(jnp.int32, (QPAIR, LANES), 1)
    key_idx = lax.broadcasted_iota(jnp.int32, (1, BAND), 1)
    heads_per_tile = LANES // HEAD_DIM
    n_pairs = rows // QPAIR
    row_in_pair = lax.broadcasted_iota(jnp.int32, (QPAIR, LANES), 0)
    oldest_hidden = jnp.where((row_in_pair >= CHUNK) & (lane < CHUNK), NEG_INF, 0.0)
    band0, before_start, oldest_mask = [], [], []
    for p in range(n_pairs):
        row0 = pos0 + p * QPAIR
        band0.append(pl.multiple_of(row0, LANES))
        before_start.append(jnp.where(key_idx >= PAD - row0, 0.0, NEG_INF))
        m0 = oldest_hidden + before_start[p][:, :LANES]
        oldest_mask.append(jnp.concatenate([m0] * heads_per_tile, axis=0))
    items = [(p, hp) for p in range(n_pairs) for hp in range(N_HEADS // heads_per_tile)]

    def scores(i):
        p, hp = items[i]
        hcols = slice(hp * LANES, (hp + 1) * LANES)
        qp = q_sc[p * QPAIR:(p + 1) * QPAIR, hcols]
        kp = k_sc[pl.ds(band0[p], BAND), hcols]
        zero = jnp.zeros_like(qp)
        q2 = jnp.concatenate([jnp.where((lane // HEAD_DIM) == hh, qp, zero)
                              for hh in range(heads_per_tile)], axis=0)
        return lax.dot_general(q2, kp, (((1,), (1,)), ((), ())), preferred_element_type=F32)

    def biased_max(i, raw):
        p, hp = items[i]
        bs = before_start[p]
        near = bias_ref[hp]
        mid = FAR_COLS + NEAR_COLS - LANES
        sc = jnp.concatenate([raw[:, :LANES] + oldest_mask[p],
                              raw[:, LANES:FAR_COLS] + bs[:, LANES:FAR_COLS],
                              raw[:, FAR_COLS:mid] + near[:, :mid - FAR_COLS] + bs[:, FAR_COLS:mid],
                              raw[:, mid:] + near[:, mid - FAR_COLS:]], axis=1)
        return sc, jnp.max(_fold_lane_tiles(sc, jnp.maximum), axis=-1, keepdims=True)

    def weighted_values(i, sc, m):
        p, hp = items[i]
        e = jnp.exp2(sc - m)
        denom = jnp.sum(_fold_lane_tiles(e, jnp.add), axis=-1, keepdims=True)
        vp = v_sc[pl.ds(band0[p], BAND), hp * LANES:(hp + 1) * LANES]
        return _dot(e.astype(BF16), vp), denom

    def finish(i, o, denom):
        p, hp = items[i]
        o = o * (1.0 / denom)
        merged = jnp.where((lane // HEAD_DIM) == 0, o[:QPAIR], o[QPAIR:])
        att_sc[p * QPAIR:(p + 1) * QPAIR, hp * LANES:(hp + 1) * LANES] = merged.astype(BF16)

    raw, biased, weighted = {}, {}, {}
    for t in range(len(items) + 3):
        if t < len(items):
            raw[t] = scores(t)
        if 0 <= t - 1 < len(items):
            biased[t - 1] = biased_max(t - 1, raw.pop(t - 1))
        if 0 <= t - 2 < len(items):
            weighted[t - 2] = weighted_values(t - 2, *biased.pop(t - 2))
        if 0 <= t - 3 < len(items):
            finish(t - 3, *weighted.pop(t - 3))

    _residual_proj_into(o_rows, h, att_sc, wo_ref)
    _conv_ffn_into(o_rows, fng_ref, fwin_ref, fcw_ref, fcb_ref, fwd_ref, carry_ref)
    y = o_rows[...]
    o_rows[...] = y * _rms_scale(y) * fing_ref[...]


def _layer1_kernel(h_ref, *refs, tile):
    params, o_ref, scratch = refs[:12], refs[12], refs[13:]
    k_sc, v_sc, carry_ref = scratch[0], scratch[1], scratch[-1]
    s_idx = pl.program_id(1)

    @pl.when(s_idx == 0)
    def _():
        carry_ref[...] = jnp.zeros_like(carry_ref)
        k_sc[0:PAD, :] = jnp.zeros((PAD, k_sc.shape[1]), BF16)
        v_sc[0:PAD, :] = jnp.zeros((PAD, v_sc.shape[1]), BF16)

    for sub in range(tile // PASS_ROWS):
        rows = pl.ds(sub * PASS_ROWS, PASS_ROWS)
        _layer1_rows(s_idx * tile + sub * PASS_ROWS, h_ref.at[0, rows], o_ref.at[0, rows],
                     *params, *scratch)


def _cast_kernel(w_ref, o_ref):
    o_ref[...] = w_ref[...].astype(o_ref.dtype)


def _to_bf16(w, layer=0):
    _, n_rows, n_cols = w.shape
    block_rows = n_rows
    while block_rows * n_cols * w.dtype.itemsize > CAST_BLOCK_BYTES and block_rows % (2 * SUBLANES) == 0:
        block_rows //= 2
    assert n_rows % block_rows == 0
    return pl.pallas_call(
        _cast_kernel,
        grid=(n_rows // block_rows,),
        in_specs=[pl.BlockSpec((None, block_rows, n_cols), lambda i: (layer, i, 0))],
        out_specs=pl.BlockSpec((block_rows, n_cols), lambda i: (i, 0)),
        out_shape=jax.ShapeDtypeStruct((n_rows, n_cols), BF16),
        compiler_params=pltpu.CompilerParams(dimension_semantics=("arbitrary",)),
        name="weight_to_bf16",
    )(w)


def _resident(arr):
    nd = arr.ndim
    return pl.BlockSpec(arr.shape, lambda b, s: (0,) * nd, pipeline_mode=pl.Buffered(1))


def _rel_bias_table(rel_bias):
    n_heads = rel_bias.shape[0]
    rel_bias = rel_bias.astype(F32)
    far = jnp.broadcast_to(rel_bias[:, 2 * REL_CLIP:], (n_heads, BAND - 1 - REL_CLIP))
    near = rel_bias[:, REL_CLIP - QPAIR + 1:2 * REL_CLIP + 1][:, ::-1]
    by_dist = jnp.concatenate([far, near], axis=1)
    length = BAND + QPAIR - 1
    padded = jnp.pad(by_dist, ((0, 0), (0, 1)))
    skew = jnp.tile(padded, (1, QPAIR))[:, :QPAIR * length].reshape(n_heads, QPAIR, length)
    table = skew[:, :, QPAIR - 1:]
    r = jnp.arange(QPAIR)[:, None]
    j = jnp.arange(BAND)[None, :]
    lo = (r // CHUNK) * CHUNK
    inside = (j >= lo) & (j < lo + PAD + CHUNK)
    return jnp.where(inside[None], table, NEG_INF)


def _pick_tile(seq):
    for n in range(PASSES_PER_STEP, 0, -1):
        if seq % (n * PASS_ROWS) == 0:
            return n * PASS_ROWS
    raise ValueError(f"sequence length {seq} must be a multiple of {PASS_ROWS}")


def kernel(x, a_norm_g, a_w_in, a_v_norm_g, a_w_s, a_b_s, a_w_out, kv_norm_g, w_kv, b_norm_g, b_w_q, b_rel_bias, b_w_o, f_norm_g, f_w_in, f_conv_w, f_conv_b, f_w_down, final_norm_g):
    B, S, D = x.shape
    assert a_norm_g.shape[0] == 1 and b_norm_g.shape[0] == 1 and f_norm_g.shape[0] == 2
    width = a_w_out.shape[1]
    hd = b_w_q.shape[2]
    d_ff = f_w_down.shape[1]
    assert hd == N_HEADS * HEAD_DIM and width == GMLP_GROUPS * LANES and d_ff % MXU_DIM == 0
    tile = _pick_tile(S)
    grid = (B, S // tile)
    row = lambda v: v.reshape(1, -1).astype(F32)
    act_spec = pl.BlockSpec((1, tile, D), lambda b, s: (b, s, 0))
    params = pltpu.CompilerParams(dimension_semantics=("arbitrary", "arbitrary"),
                                  vmem_limit_bytes=VMEM_LIMIT_BYTES)

    def ffn_args(l):
        return (row(f_norm_g[l]), _to_bf16(f_w_in, l), f_conv_w[l].astype(F32),
                row(f_conv_b[l]), _to_bf16(f_w_down, l))

    bias_rows = jnp.repeat(a_b_s[0].T.astype(F32), LANES, axis=1)
    args0 = (row(a_norm_g[0]), _to_bf16(a_w_in), row(a_v_norm_g[0]), a_w_s[0].astype(F32),
             bias_rows, _to_bf16(a_w_out)) + ffn_args(0)
    h1 = pl.pallas_call(
        functools.partial(_layer0_kernel, tile=tile),
        grid=grid,
        in_specs=[act_spec] + [_resident(a) for a in args0],
        out_specs=act_spec,
        out_shape=jax.ShapeDtypeStruct((B, S, D), F32),
        scratch_shapes=[pltpu.VMEM((PASS_ROWS, width), F32),
                        pltpu.VMEM((PASS_ROWS, width), BF16),
                        pltpu.VMEM((CONV_CARRY_ROWS, 2 * d_ff), F32)],
        compiler_params=params,
        name="layer0_gmlp_convffn",
    )(x, *args0)

    rel = b_rel_bias[0].astype(F32)
    near_bias = _rel_bias_table((rel - rel[:, 2 * REL_CLIP:]) * LOG2E)[:, :, FAR_COLS:]
    args1 = (row(kv_norm_g), row(b_norm_g[0]), _to_bf16(w_kv[None]), _to_bf16(b_w_q),
             near_bias.reshape(-1, 2 * QPAIR, NEAR_COLS), _to_bf16(b_w_o)) \
        + ffn_args(1) + (row(final_norm_g),)
    out = pl.pallas_call(
        functools.partial(_layer1_kernel, tile=tile),
        grid=grid,
        in_specs=[act_spec] + [_resident(a) for a in args1],
        out_specs=act_spec,
        out_shape=jax.ShapeDtypeStruct((B, S, D), F32),
        scratch_shapes=[pltpu.VMEM((PAD + S, hd), BF16),
                        pltpu.VMEM((PAD + S, hd), BF16),
                        pltpu.VMEM((PASS_ROWS, hd), BF16),
                        pltpu.VMEM((PASS_ROWS, hd), BF16),
                        pltpu.VMEM((CONV_CARRY_ROWS, 2 * d_ff), F32)],
        compiler_params=params,
        name="layer1_attn_convffn",
    )(h1, *args1)
    return out.astype(x.dtype)
```
